```python
import jax
import jax.numpy as jnp
from jax import lax
import numpy as np

D_MODEL = 1024
BATCH = 32
SEQ = 2048
DEPTH = 1

HEAD_DIM = 64
RW_HEADS = 8
RW_WIDTH = RW_HEADS * HEAD_DIM
NSA_HEADS = 8
NSA_KV_HEADS = 2
NSA_GROUP = NSA_HEADS // NSA_KV_HEADS
NSA_WIDTH = NSA_HEADS * HEAD_DIM
NSA_KV_WIDTH = NSA_KV_HEADS * HEAD_DIM
MIX_WIDTH = RW_WIDTH + NSA_WIDTH
RW_DECAY_LORA = 64
RW_ICLR_LORA = 64
RW_GATE_LORA = 128
CMP_LEN = 32
CMP_STRIDE = 16
CMP_HIDDEN = 256
SEL_BLOCK = 64
SEL_TOPK = 8
SEL_Q_BLOCK = 64
WINDOW = 256
Q_BLOCK = 128
ROPE_THETA = 10000.0
N_EXPERTS = 32
TOP_K = 4
D_FF = 1024
SWIGLU_LIMIT = 7.0
SWIGLU_ALPHA = 1.702
MOE_ROW_BLOCK = 128
NORM_EPS = 1e-6
LNX_EPS = 64e-5
FORCE_SCORE = 1e6
NEG_INF = -1e30
IN_SPLITS = (RW_WIDTH, RW_WIDTH, RW_WIDTH, NSA_WIDTH, NSA_KV_WIDTH, NSA_KV_WIDTH, NSA_KV_WIDTH, NSA_KV_WIDTH, NSA_KV_WIDTH, NSA_KV_WIDTH, 3 * NSA_HEADS)
N_IN = 3 * RW_WIDTH + NSA_WIDTH + 6 * NSA_KV_WIDTH + 3 * NSA_HEADS

kernel_name = 'hybrid_rwkv7_nsa_moe'


def rmsnorm(x, g, eps=NORM_EPS):
    xf = x.astype(jnp.float32)
    y = xf * lax.rsqrt(jnp.mean(xf * xf, axis=-1, keepdims=True) + eps)
    return (y * g.astype(jnp.float32)).astype(x.dtype)


def token_shift(t):
    return jnp.pad(t, ((0, 0), (1, 0), (0, 0)))[:, :-1]


def rope(t, pos):
    half = t.shape[-1] // 2
    inv = ROPE_THETA ** (-jnp.arange(half, dtype=jnp.float32) / half)
    ang = jnp.asarray(pos, jnp.float32)[:, None] * inv[None, :]
    cos = jnp.cos(ang)[None, :, None, :]
    sin = jnp.sin(ang)[None, :, None, :]
    tf = t.astype(jnp.float32)
    t1, t2 = tf[..., :half], tf[..., half:]
    return jnp.concatenate([t1 * cos - t2 * sin, t2 * cos + t1 * sin], axis=-1).astype(t.dtype)


def masked_softmax(s, mask):
    s = jnp.where(mask, s.astype(jnp.float32), NEG_INF)
    m = jnp.max(s, axis=-1, keepdims=True)
    e = jnp.exp(s - m) * mask
    d = jnp.sum(e, axis=-1, keepdims=True)
    return e / jnp.where(d > 0, d, 1.0)


def rwkv7_mix(xn, r, k, v, mu_rkv, mu_wag, w0, w1, w2, a0, a1, a2, g1, g2, k_k, k_a, r_k, lnx_g, lnx_b):
    B, S, _ = xn.shape
    H, N = RW_HEADS, HEAD_DIM
    f32 = jnp.float32
    r = r + mu_rkv[0] * (token_shift(r) - r)
    k = k + mu_rkv[1] * (token_shift(k) - k)
    v = v + mu_rkv[2] * (token_shift(v) - v)
    dx = token_shift(xn) - xn
    xw = xn + mu_wag[0] * dx
    xa = xn + mu_wag[1] * dx
    xg = xn + mu_wag[2] * dx
    w = -jax.nn.softplus(-(w0 + jnp.tanh(xw @ w1) @ w2)) - 0.5
    a = jax.nn.sigmoid(a0 + (xa @ a1) @ a2)
    g = jax.nn.sigmoid(xg @ g1) @ g2
    kk = (k * k_k).astype(f32).reshape(B, S, H, N)
    kk = kk / jnp.maximum(jnp.sqrt(jnp.sum(kk * kk, axis=-1, keepdims=True)), 1e-12)
    k = k * (1.0 + (a - 1.0) * k_a)
    decay = jnp.exp(-jnp.exp(w.astype(f32))).reshape(B, S, H, N)
    rh = r.astype(f32).reshape(B, S, H, N)
    kh = k.astype(f32).reshape(B, S, H, N)
    vh = v.astype(f32).reshape(B, S, H, N)
    ah = a.astype(f32).reshape(B, S, H, N)
    a_vec = -kk
    b_vec = kk * ah

    def step(state, inp):
        r_t, w_t, k_t, v_t, av_t, bv_t = inp
        sa = jnp.einsum('bhij,bhj->bhi', state, av_t)
        state = state * w_t[:, :, None, :] + sa[..., :, None] * bv_t[..., None, :] + v_t[..., :, None] * k_t[..., None, :]
        y_t = jnp.einsum('bhij,bhj->bhi', state, r_t)
        return state, y_t

    xs = tuple(jnp.swapaxes(t, 0, 1) for t in (rh, decay, kh, vh, a_vec, b_vec))
    _, y = lax.scan(step, jnp.zeros((B, H, N, N), f32), xs)
    y = jnp.swapaxes(y, 0, 1)
    mu = jnp.mean(y, axis=-1, keepdims=True)
    var = jnp.mean(jnp.square(y - mu), axis=-1, keepdims=True)
    y = ((y - mu) * lax.rsqrt(var + LNX_EPS)).reshape(B, S, RW_WIDTH)
    y = y * lnx_g.astype(f32) + lnx_b.astype(f32)
    bonus = jnp.sum(rh * kh * r_k.astype(f32), axis=-1, keepdims=True) * vh
    y = (y + bonus.reshape(B, S, RW_WIDTH)) * g.astype(f32)
    return y.astype(xn.dtype)


def nsa_mix(q, kc, vc, ks, vs, kw, vw, gate_logits, qn_g, kn_g, pe, cmp_w1, cmp_w2):
    B, S, _ = q.shape
    H, Hk, G, d = NSA_HEADS, NSA_KV_HEADS, NSA_GROUP, HEAD_DIM
    f32 = jnp.float32
    scale = HEAD_DIM ** -0.5
    pos = np.arange(S)
    q = rmsnorm(q.reshape(B, S, H, d), qn_g)
    q_rot = rope(q, pos)
    kc, vc, ks, vs, kw, vw = (t.reshape(B, S, Hk, d) for t in (kc, vc, ks, vs, kw, vw))

    n_cmp = (S - CMP_LEN) // CMP_STRIDE + 1
    cmp_start = np.arange(n_cmp) * CMP_STRIDE
    blk_idx = cmp_start[:, None] + np.arange(CMP_LEN)[None, :]

    def compress(t, pe_i, w1_i, w2_i):
        blocks = t[:, blk_idx] + pe_i[None, None, :, None, :]
        hid = jax.nn.gelu(jnp.einsum('bnlkd,ldf->bnkf', blocks, w1_i))
        return jnp.einsum('bnkf,fd->bnkd', hid, w2_i)

    k_cmp = rmsnorm(compress(kc, pe[0], cmp_w1[0], cmp_w2[0]), kn_g[0])
    v_cmp = compress(vc, pe[1], cmp_w1[1], cmp_w2[1])
    qg = q.reshape(B, S, Hk, G, d)
    s_cmp = jnp.einsum('bskgd,bnkd->bkgsn', qg, k_cmp) * scale
    cmp_mask = (cmp_start + CMP_LEN - 1)[None, :] <= pos[:, None]
    p_cmp = masked_softmax(s_cmp, cmp_mask)
    o_cmp = jnp.einsum('bkgsn,bnkd->bskgd', p_cmp, v_cmp.astype(f32)).reshape(B, S, H, d)

    n_sel = S // SEL_BLOCK
    sel_start = np.arange(n_sel) * SEL_BLOCK
    overlap = ((cmp_start[:, None] < sel_start[None, :] + SEL_BLOCK) & (cmp_start[:, None] + CMP_LEN > sel_start[None, :])).astype(np.float32)
    imp = jnp.einsum('bkgsn,nj->bksj', p_cmp, jnp.asarray(overlap))
    cur = pos // SEL_BLOCK
    jj = np.arange(n_sel)
    forced = (jj[None, :] == 0) | (jj[None, :] == cur[:, None]) | (jj[None, :] == cur[:, None] - 1)
    imp = jnp.where(forced, FORCE_SCORE, imp)
    imp = jnp.where(jj[None, :] > cur[:, None], -FORCE_SCORE, imp)
    n_top = min(SEL_TOPK, n_sel)
    _, sel_idx = lax.top_k(imp, n_top)

    ks_r = rope(rmsnorm(ks, kn_g[1]), pos)
    k_blocks = ks_r.reshape(B, n_sel, SEL_BLOCK, Hk, d).transpose(0, 3, 1, 2, 4)
    v_blocks = vs.reshape(B, n_sel, SEL_BLOCK, Hk, d).transpose(0, 3, 1, 2, 4)
    nqs = S // SEL_Q_BLOCK
    q_chunks = q_rot.reshape(B, nqs, SEL_Q_BLOCK, Hk, G, d).transpose(1, 0, 3, 4, 2, 5)
    idx_chunks = sel_idx.reshape(B, Hk, nqs, SEL_Q_BLOCK, n_top).transpose(2, 0, 1, 3, 4)
    qpos_chunks = jnp.asarray(pos.reshape(nqs, SEL_Q_BLOCK))
    bi = jnp.arange(B)[:, None, None, None]
    hi = jnp.arange(Hk)[None, :, None, None]

    def sel_block(args):
        qc, ic, qp = args
        kg = k_blocks[bi, hi, ic]
        vg = v_blocks[bi, hi, ic]
        s = jnp.einsum('bkgqd,bkqnld->bkgqnl', qc, kg) * scale
        kpos = ic[..., None] * SEL_BLOCK + jnp.arange(SEL_BLOCK)
        mask = kpos <= qp[None, None, :, None, None]
        p = masked_softmax(s.reshape(B, Hk, G, SEL_Q_BLOCK, -1), mask.reshape(B, Hk, 1, SEL_Q_BLOCK, -1))
        return jnp.einsum('bkgqm,bkqmd->bkgqd', p, vg.reshape(B, Hk, SEL_Q_BLOCK, -1, d).astype(f32))

    o_sel = lax.map(sel_block, (q_chunks, idx_chunks, qpos_chunks))
    o_sel = o_sel.transpose(1, 0, 4, 2, 3, 5).reshape(B, S, H, d)

    kw_r = rope(rmsnorm(kw, kn_g[2]), pos)
    nb = S // Q_BLOCK
    n_band = WINDOW // Q_BLOCK + 1

    def band(t):
        tb = t.reshape(B, nb, Q_BLOCK, Hk, d).transpose(0, 3, 1, 2, 4)
        tp = jnp.pad(tb, ((0, 0), (0, 0), (n_band - 1, 0), (0, 0), (0, 0)))
        return jnp.concatenate([tp[:, :, i:i + nb] for i in range(n_band)], axis=3)

    k_band = band(kw_r)
    v_band = band(vw)
    qw = q_rot.reshape(B, nb, Q_BLOCK, Hk, G, d).transpose(0, 3, 4, 1, 2, 5)
    s_win = jnp.einsum('bkgiqd,bkird->bkgiqr', qw, k_band) * scale
    qpos = pos.reshape(nb, Q_BLOCK)
    kpos = (np.arange(nb)[:, None] - (n_band - 1)) * Q_BLOCK + np.arange(n_band * Q_BLOCK)[None, :]
    win_mask = (kpos[:, None, :] <= qpos[:, :, None]) & (kpos[:, None, :] > qpos[:, :, None] - WINDOW) & (kpos[:, None, :] >= 0)
    p_win = masked_softmax(s_win, win_mask)
    o_win = jnp.einsum('bkgiqr,bkird->bkgiqd', p_win, v_band.astype(f32))
    o_win = o_win.transpose(0, 3, 4, 1, 2, 5).reshape(B, S, H, d)

    gates = jax.nn.sigmoid(gate_logits.astype(f32)).reshape(B, S, H, 3)
    o = gates[..., 0:1] * o_cmp + gates[..., 1:2] * o_sel + gates[..., 2:3] * o_win
    return o.reshape(B, S, NSA_WIDTH).astype(q.dtype)


def moe_ffn(h, wr, br, w1, b1, w2, b2):
    B, S, D = h.shape
    T = B * S
    M = MOE_ROW_BLOCK
    hf = h.reshape(T, D)
    logits = (hf @ wr + br).astype(jnp.float32)
    top_v, top_e = lax.top_k(logits, TOP_K)
    gate = jax.nn.softmax(top_v, axis=-1)
    flat_e = top_e.reshape(-1)
    flat_t = jnp.arange(T * TOP_K) // TOP_K
    flat_g = gate.reshape(-1)
    order = jnp.argsort(flat_e)
    se, st, sg = flat_e[order], flat_t[order], flat_g[order]
    counts = jnp.bincount(flat_e, length=N_EXPERTS)
    padded = ((counts + M - 1) // M) * M
    pad_end = jnp.cumsum(padded)
    pad_start = pad_end - padded
    sort_start = jnp.cumsum(counts) - counts
    dest = pad_start[se] + jnp.arange(T * TOP_K) - sort_start[se]
    n_rows = T * TOP_K + N_EXPERTS * M
    n_blk = n_rows // M
    row_tok = jnp.full((n_rows,), T, jnp.int32).at[dest].set(st)
    row_gate = jnp.zeros((n_rows,), jnp.float32).at[dest].set(sg)
    blk_e = jnp.minimum(jnp.searchsorted(pad_end, jnp.arange(n_blk) * M, side='right'), N_EXPERTS - 1)
    h_pad = jnp.concatenate([hf, jnp.zeros((1, D), hf.dtype)], axis=0)

    def expert_block(args):
        tok, g, e = args
        u = h_pad[tok] @ w1[e] + b1[e]
        glu = jnp.minimum(u[:, :D_FF], SWIGLU_LIMIT)
        lin = jnp.clip(u[:, D_FF:], -SWIGLU_LIMIT, SWIGLU_LIMIT)
        act = glu * jax.nn.sigmoid(SWIGLU_ALPHA * glu) * (lin + 1.0)
        y = act @ w2[e] + b2[e]
        return y * g[:, None].astype(y.dtype)

    y_rows = lax.map(expert_block, (row_tok.reshape(n_blk, M), row_gate.reshape(n_blk, M), blk_e))
    out = jnp.zeros((T + 1, D), y_rows.dtype).at[row_tok].add(y_rows.reshape(n_rows, D))
    return out[:T].reshape(B, S, D).astype(h.dtype)


def hybrid_layer(x, ln1_g, w_in, rw_mu_rkv, rw_mu_wag, rw_w0, rw_w1, rw_w2, rw_a0, rw_a1, rw_a2, rw_g1, rw_g2, rw_k_k, rw_k_a, rw_r_k, rw_lnx_g, rw_lnx_b, nsa_qn_g, nsa_kn_g, nsa_pe, nsa_cmp_w1, nsa_cmp_w2, w_out, ln2_g, moe_wr, moe_br, moe_w1, moe_b1, moe_w2, moe_b2):
    xn = rmsnorm(x, ln1_g)
    proj = xn @ w_in
    cuts = np.cumsum(np.array(IN_SPLITS))[:-1].tolist()
    r, k, v, q, kc, vc, ks, vs, kw, vw, gl = jnp.split(proj, cuts, axis=-1)
    y_rw = rwkv7_mix(xn, r, k, v, rw_mu_rkv, rw_mu_wag, rw_w0, rw_w1, rw_w2, rw_a0, rw_a1, rw_a2, rw_g1, rw_g2, rw_k_k, rw_k_a, rw_r_k, rw_lnx_g, rw_lnx_b)
    y_nsa = nsa_mix(q, kc, vc, ks, vs, kw, vw, gl, nsa_qn_g, nsa_kn_g, nsa_pe, nsa_cmp_w1, nsa_cmp_w2)
    x = x + jnp.concatenate([y_rw, y_nsa], axis=-1) @ w_out
    x = x + moe_ffn(rmsnorm(x, ln2_g), moe_wr, moe_br, moe_w1, moe_b1, moe_w2, moe_b2)
    return x


def setup_inputs(seed: int = 0) -> dict:
    key = jax.random.key(seed)
    kk = jax.random.split(key, 32)
    f32 = jnp.float32
    L, D = DEPTH, D_MODEL

    def nrm(k, shape, s):
        return jax.random.normal(k, shape, f32) * s

    def uni(k, shape, lo, hi):
        return jax.random.uniform(k, shape, f32, lo, hi)

    return {
        'x': nrm(kk[0], (BATCH, SEQ, D), 1.0),
        'ln1_g': 1.0 + nrm(kk[1], (L, D), 0.02),
        'w_in': nrm(kk[2], (L, D, N_IN), D ** -0.5),
        'rw_mu_rkv': uni(kk[3], (L, 3, RW_WIDTH), 0.0, 1.0),
        'rw_mu_wag': uni(kk[4], (L, 3, D), 0.0, 1.0),
        'rw_w0': uni(kk[5], (L, RW_WIDTH), -4.0, 1.0),
        'rw_w1': nrm(kk[6], (L, D, RW_DECAY_LORA), 0.5 * D ** -0.5),
        'rw_w2': nrm(kk[7], (L, RW_DECAY_LORA, RW_WIDTH), 0.5 * RW_DECAY_LORA ** -0.5),
        'rw_a0': nrm(kk[8], (L, RW_WIDTH), 0.1),
        'rw_a1': nrm(kk[9], (L, D, RW_ICLR_LORA), 0.5 * D ** -0.5),
        'rw_a2': nrm(kk[10], (L, RW_ICLR_LORA, RW_WIDTH), 0.5 * RW_ICLR_LORA ** -0.5),
        'rw_g1': nrm(kk[11], (L, D, RW_GATE_LORA), D ** -0.5),
        'rw_g2': nrm(kk[12], (L, RW_GATE_LORA, RW_WIDTH), RW_GATE_LORA ** -0.5),
        'rw_k_k': 0.85 + nrm(kk[13], (L, RW_WIDTH), 0.05),
        'rw_k_a': 1.0 + nrm(kk[14], (L, RW_WIDTH), 0.05),
        'rw_r_k': nrm(kk[15], (L, RW_HEADS, HEAD_DIM), 0.1),
        'rw_lnx_g': 1.0 + nrm(kk[16], (L, RW_WIDTH), 0.02),
        'rw_lnx_b': nrm(kk[17], (L, RW_WIDTH), 0.01),
        'nsa_qn_g': 1.0 + nrm(kk[18], (L, HEAD_DIM), 0.02),
        'nsa_kn_g': 1.0 + nrm(kk[19], (L, 3, HEAD_DIM), 0.02),
        'nsa_pe': nrm(kk[20], (L, 2, CMP_LEN, HEAD_DIM), 0.02),
        'nsa_cmp_w1': nrm(kk[21], (L, 2, CMP_LEN, HEAD_DIM, CMP_HIDDEN), (CMP_LEN * HEAD_DIM) ** -0.5),
        'nsa_cmp_w2': nrm(kk[22], (L, 2, CMP_HIDDEN, HEAD_DIM), CMP_HIDDEN ** -0.5),
        'w_out': nrm(kk[23], (L, MIX_WIDTH, D), MIX_WIDTH ** -0.5),
        'ln2_g': 1.0 + nrm(kk[24], (L, D), 0.02),
        'moe_wr': nrm(kk[25], (L, D, N_EXPERTS), D ** -0.5),
        'moe_br': nrm(kk[26], (L, N_EXPERTS), 0.01),
        'moe_w1': nrm(kk[27], (L, N_EXPERTS, D, 2 * D_FF), D ** -0.5),
        'moe_b1': nrm(kk[28], (L, N_EXPERTS, 2 * D_FF), 0.01),
        'moe_w2': nrm(kk[29], (L, N_EXPERTS, D_FF, D), D_FF ** -0.5),
        'moe_b2': nrm(kk[30], (L, N_EXPERTS, D), 0.01),
    }


def reference(x, ln1_g, w_in, rw_mu_rkv, rw_mu_wag, rw_w0, rw_w1, rw_w2, rw_a0, rw_a1, rw_a2, rw_g1, rw_g2, rw_k_k, rw_k_a, rw_r_k, rw_lnx_g, rw_lnx_b, nsa_qn_g, nsa_kn_g, nsa_pe, nsa_cmp_w1, nsa_cmp_w2, w_out, ln2_g, moe_wr, moe_br, moe_w1, moe_b1, moe_w2, moe_b2):
    for l in range(DEPTH):
        x = hybrid_layer(x, ln1_g[l], w_in[l], rw_mu_rkv[l], rw_mu_wag[l], rw_w0[l], rw_w1[l], rw_w2[l], rw_a0[l], rw_a1[l], rw_a2[l], rw_g1[l], rw_g2[l], rw_k_k[l], rw_k_a[l], rw_r_k[l], rw_lnx_g[l], rw_lnx_b[l], nsa_qn_g[l], nsa_kn_g[l], nsa_pe[l], nsa_cmp_w1[l], nsa_cmp_w2[l], w_out[l], ln2_g[l], moe_wr[l], moe_br[l], moe_w1[l], moe_b1[l], moe_w2[l], moe_b2[l])
    return x
```

```python
import functools

import numpy as np
import jax
import jax.numpy as jnp
from jax import lax
from jax.experimental import pallas as pl
from jax.experimental.pallas import tpu as pltpu

F32 = jnp.float32
BF16 = jnp.bfloat16
HI = lax.Precision.HIGHEST

HEAD_DIM = 64
RW_HEADS = 8
RW_WIDTH = RW_HEADS * HEAD_DIM
NSA_HEADS = 8
NSA_KV_HEADS = 2
NSA_GROUP = NSA_HEADS // NSA_KV_HEADS
NSA_WIDTH = NSA_HEADS * HEAD_DIM
NSA_KV_WIDTH = NSA_KV_HEADS * HEAD_DIM
LORA_W, LORA_A, LORA_G = 64, 64, 128
LORA_ALL = LORA_W + LORA_A + LORA_G
CMP_LEN = 32
CMP_STRIDE = 16
CMP_HIDDEN = 256
SEL_BLOCK = 64
SEL_TOPK = 8
WINDOW = 256
ROPE_THETA = 10000.0
N_EXPERTS = 32
TOP_K = 4
SWIGLU_LIMIT = 7.0
SWIGLU_ALPHA = 1.702
NORM_EPS = 1e-6
LNX_EPS = 64e-5
FORCE_SCORE = 1e6
NEG_INF = -1e30

GL_PAD = 128
ROUTE_PAD = 128
ATT_TILE = 128
MOE_ROWS = 256
VMEM_LIMIT = 56 * 1024 * 1024


def _cparams(*sem):
    return pltpu.CompilerParams(dimension_semantics=sem, vmem_limit_bytes=VMEM_LIMIT)


def _nt_dot(a, b, precision=None):
    return lax.dot_general(a, b, (((1,), (1,)), ((), ())), precision=precision,
                           preferred_element_type=F32)


IN_WIDTHS = (3 * RW_WIDTH, NSA_WIDTH, 6 * NSA_KV_WIDTH, GL_PAD, 2 * LORA_ALL)


def _inproj_kernel(x_ref, g_ref, w_ref, rkv_ref, q_ref, kv_ref, gl_ref, lora_ref):
    x = x_ref[...]
    ms = jnp.mean(x * x, axis=-1, keepdims=True)
    xn = (x * lax.rsqrt(ms + NORM_EPS) * g_ref[...]).astype(BF16)
    off = 0
    for ref in (rkv_ref, q_ref, kv_ref, gl_ref, lora_ref):
        n = ref.shape[-1]
        ref[...] = jnp.dot(xn, w_ref[:, off:off + n], preferred_element_type=F32)
        off += n


def _inproj(x2, ln1_g, w_all, tm=512):
    t, d = x2.shape
    n_all = w_all.shape[1]
    row = lambda n: pl.BlockSpec((tm, n), lambda i: (i, 0))
    return pl.pallas_call(
        _inproj_kernel,
        grid=(t // tm,),
        in_specs=[row(d), pl.BlockSpec((1, d), lambda i: (0, 0)),
                  pl.BlockSpec((d, n_all), lambda i: (0, 0))],
        out_specs=[row(n) for n in IN_WIDTHS],
        out_shape=[jax.ShapeDtypeStruct((t, n), F32) for n in IN_WIDTHS],
        compiler_params=_cparams("parallel"),
        name="inproj",
    )(x2, ln1_g.reshape(1, d), w_all)


def _shift_rows(x, prev_row):
    rolled = pltpu.roll(x, 1, axis=0)
    first = lax.broadcasted_iota(jnp.int32, x.shape, 0) == 0
    return jnp.where(first, prev_row, rolled)


def _softplus(z):
    return jnp.maximum(z, 0.0) + jnp.log(1.0 + jnp.exp(-jnp.abs(z)))


def _sigmoid(z):
    return 1.0 / (1.0 + jnp.exp(-z))


def _rwkv_prep_kernel(rkv_ref, rkvp_ref, lo_ref, lop_ref, mu_ref, w0_ref, w2_ref, a0_ref, a2_ref,
                      g2_ref, kk_ref, ka_ref, r_out, w_out, k_out, v_out, kkr_out, a_out, g_out):
    first_tile = pl.program_id(1) == 0
    x = rkv_ref[...]
    prev = jnp.where(first_tile, 0.0, rkvp_ref[7:8, :])
    mixed = x + mu_ref[...] * (_shift_rows(x, prev) - x)
    c = RW_WIDTH
    r, k, v = mixed[:, :c], mixed[:, c:2 * c], mixed[:, 2 * c:]

    lo = lo_ref[...]
    lprev = jnp.where(first_tile, 0.0, lop_ref[7:8, :])
    pre = lo[:, :LORA_ALL] + _shift_rows(lo, lprev)[:, LORA_ALL:]
    hw = jnp.tanh(pre[:, :LORA_W])
    ha = pre[:, LORA_W:LORA_W + LORA_A]
    hg = _sigmoid(pre[:, LORA_W + LORA_A:])
    lw = w0_ref[...] + jnp.dot(hw, w2_ref[...], precision=HI, preferred_element_type=F32)
    w = -_softplus(-lw) - 0.5
    a = _sigmoid(a0_ref[...] + jnp.dot(ha, a2_ref[...], precision=HI, preferred_element_type=F32))
    g = jnp.dot(hg, g2_ref[...], precision=HI, preferred_element_type=F32)

    r_out[...] = r
    w_out[...] = jnp.exp(-jnp.exp(w))
    k_out[...] = k * (1.0 + (a - 1.0) * ka_ref[...])
    v_out[...] = v
    kkr_out[...] = k * kk_ref[...]
    a_out[...] = a
    g_out[...] = g


def _rwkv_prep(rkv, lora, batch, seq, mu_rkv, w0, w2, a0, a2, g2, k_k, k_a, ts=512):
    t = rkv.shape[0]
    ts = min(ts, seq)
    ns = seq // ts
    c = RW_WIDTH
    cur = lambda n: pl.BlockSpec((ts, n), lambda b, s: (b * ns + s, 0))
    prev = lambda n: pl.BlockSpec((8, n), lambda b, s: (jnp.maximum((b * ns + s) * (ts // 8) - 1, 0), 0))
    full = lambda a: pl.BlockSpec(a.shape, lambda b, s: (0, 0))
    params = [mu_rkv.reshape(1, 3 * c), w0.reshape(1, c), w2, a0.reshape(1, c), a2, g2,
              k_k.reshape(1, c), k_a.reshape(1, c)]
    return pl.pallas_call(
        _rwkv_prep_kernel,
        grid=(batch, ns),
        in_specs=[cur(3 * c), prev(3 * c), cur(2 * LORA_ALL), prev(2 * LORA_ALL)] + [full(p) for p in params],
        out_specs=[cur(c)] * 7,
        out_shape=[jax.ShapeDtypeStruct((t, c), F32)] * 7,
        compiler_params=_cparams("parallel", "arbitrary"),
        name="rwkv_prep",
    )(rkv, rkv, lora, lora, *params)


def _rwkv_scan_kernel(r_ref, w_ref, k_ref, v_ref, kk_ref, a_ref, rk_ref, lng_ref, lnb_ref, y_ref,
                      s_ref, av_ref, bv_ref, wr_ref, *, steps):
    n = HEAD_DIM

    @pl.when(pl.program_id(0) == 0)
    def _():
        s_ref[...] = jnp.zeros_like(s_ref)

    def step(t, carry):
        r, w, k, a = r_ref[t], w_ref[t], k_ref[t], a_ref[t]
        kk = kk_ref[t]
        nrm = jnp.sqrt(jnp.sum(kk * kk, axis=0, keepdims=True))
        kk = kk / jnp.maximum(nrm, 1e-12)
        bv = kk * a
        av_ref[...] = -kk
        bv_ref[...] = bv
        wr_ref[...] = w * r
        br = jnp.sum(bv * r, axis=0, keepdims=True)
        kr = jnp.sum(k * r, axis=0, keepdims=True)

        def row(i, c):
            si = s_ref[i]
            vi = v_ref[t, pl.ds(i, 1), :]
            sa = jnp.sum(si * av_ref[...], axis=0, keepdims=True)
            yp = jnp.sum(si * wr_ref[...], axis=0, keepdims=True)
            s_ref[i] = si * w_ref[t] + sa * bv_ref[...] + vi * k_ref[t]
            y_ref[t, pl.ds(i, 1), :] = yp + sa * br + vi * kr
            return c

        lax.fori_loop(0, n, row, 0)

        y = y_ref[t]
        mu = jnp.mean(y, axis=0, keepdims=True)
        var = jnp.mean(jnp.square(y - mu), axis=0, keepdims=True)
        yn = (y - mu) * lax.rsqrt(var + LNX_EPS) * lng_ref[...] + lnb_ref[...]
        bonus = jnp.sum(r * k * rk_ref[...], axis=0, keepdims=True) * v_ref[t]
        y_ref[t] = yn + bonus
        return carry

    lax.fori_loop(0, steps, step, 0)


def _rwkv_scan(r, w, k, v, kk, a, rk_t, lng_t, lnb_t, tc=16):
    seq, n, lanes = r.shape
    blk = pl.BlockSpec((tc, n, lanes), lambda c: (c, 0, 0))
    par = pl.BlockSpec((n, lanes), lambda c: (0, 0))
    return pl.pallas_call(
        functools.partial(_rwkv_scan_kernel, steps=tc),
        grid=(seq // tc,),
        in_specs=[blk] * 6 + [par] * 3,
        out_specs=blk,
        out_shape=jax.ShapeDtypeStruct((seq, n, lanes), F32),
        scratch_shapes=[pltpu.VMEM((n, n, lanes), F32)] + [pltpu.VMEM((n, lanes), F32)] * 3,
        compiler_params=_cparams("arbitrary"),
        name="rwkv_scan",
    )(r, w, k, v, kk, a, rk_t, lng_t, lnb_t)


def _to_scan_layout(x, batch, seq):
    return x.reshape(batch, seq, RW_HEADS, HEAD_DIM).transpose(1, 3, 0, 2).reshape(seq, HEAD_DIM, batch * RW_HEADS)


def _from_scan_layout(y, batch, seq):
    return y.reshape(seq, HEAD_DIM, batch, RW_HEADS).transpose(2, 0, 3, 1).reshape(batch * seq, RW_WIDTH)


def _head_param_to_lanes(p, batch):
    return jnp.tile(p.reshape(RW_HEADS, HEAD_DIM).T, (1, batch))


def _rope_tables(seq):
    half = HEAD_DIM // 2
    inv = ROPE_THETA ** (-jnp.arange(half, dtype=F32) / half)
    ang = jnp.arange(seq, dtype=F32)[:, None] * inv[None, :]
    cos, sin = jnp.cos(ang), jnp.sin(ang)
    return jnp.concatenate([cos, cos], -1), jnp.concatenate([-sin, sin], -1)


def _head_norm(xh, g):
    ms = jnp.mean(xh * xh, axis=-1, keepdims=True)
    return xh * lax.rsqrt(ms + NORM_EPS) * g


def _rope(xh, cos, sin):
    half = HEAD_DIM // 2
    rot = jnp.concatenate([xh[:, half:], xh[:, :half]], axis=-1)
    return xh * cos + rot * sin


def _nsa_prep_kernel(q_ref, kv_ref, cos_ref, sin_ref, qg_ref, kg_ref, qn_out, qr_out, kc_out, vc_out,
                     ks_out, kw_out):
    d = HEAD_DIM
    cos, sin = cos_ref[...], sin_ref[...]
    q = q_ref[...]
    for h in range(NSA_HEADS):
        qn = _head_norm(q[:, h * d:(h + 1) * d], qg_ref[...])
        qn_out[0, h] = qn
        qr_out[0, h] = _rope(qn, cos, sin).astype(qr_out.dtype)
    kv = kv_ref[...]
    for h in range(NSA_KV_HEADS):
        piece = lambda j: kv[:, (j * NSA_KV_HEADS + h) * d:(j * NSA_KV_HEADS + h + 1) * d]
        kc_out[0, h] = piece(0)
        vc_out[0, h] = piece(1)
        ks_out[0, h] = _rope(_head_norm(piece(2), kg_ref[1:2, :]), cos, sin).astype(ks_out.dtype)
        kw_out[0, h] = _rope(_head_norm(piece(4), kg_ref[2:3, :]), cos, sin).astype(kw_out.dtype)


def _nsa_prep(q, kv, batch, seq, qn_g, kn_g, ts=512):
    ts = min(ts, seq)
    ns = seq // ts
    d = HEAD_DIM
    cos, sin = _rope_tables(seq)
    cur = lambda n: pl.BlockSpec((ts, n), lambda b, s: (b * ns + s, 0))
    tab = pl.BlockSpec((ts, d), lambda b, s: (s, 0))
    heads = lambda nh: pl.BlockSpec((1, nh, ts, d), lambda b, s: (b, 0, s, 0))
    shp = lambda nh, dt: jax.ShapeDtypeStruct((batch, nh, seq, d), dt)
    return pl.pallas_call(
        _nsa_prep_kernel,
        grid=(batch, ns),
        in_specs=[cur(NSA_WIDTH), cur(6 * NSA_KV_WIDTH), tab, tab,
                  pl.BlockSpec((1, d), lambda b, s: (0, 0)), pl.BlockSpec((3, d), lambda b, s: (0, 0))],
        out_specs=[heads(NSA_HEADS), heads(NSA_HEADS)] + [heads(NSA_KV_HEADS)] * 4,
        out_shape=[shp(NSA_HEADS, F32), shp(NSA_HEADS, BF16), shp(NSA_KV_HEADS, F32), shp(NSA_KV_HEADS, F32),
                   shp(NSA_KV_HEADS, BF16), shp(NSA_KV_HEADS, BF16)],
        compiler_params=_cparams("parallel", "parallel"),
        name="nsa_prep",
    )(q, kv, cos, sin, qn_g.reshape(1, d), kn_g)


def _gelu_tanh(x):
    return 0.5 * x * (1.0 + jnp.tanh(np.sqrt(2.0 / np.pi).astype(np.float32) * (x + 0.044715 * (x * x * x))))


def _compress_hidden(c_ref, pe_ref, w1_ref, which):
    c = c_ref[0, 0]
    lo = jnp.dot(c + pe_ref[which, 0:1, :], w1_ref[which, 0], precision=HI, preferred_element_type=F32)
    hi = jnp.dot(c + pe_ref[which, 1:2, :], w1_ref[which, 1], precision=HI, preferred_element_type=F32)
    return _gelu_tanh(lo + pltpu.roll(hi, c.shape[0] - 1, axis=0))


def _nsa_cmp_kernel(kc_ref, vc_ref, pe_ref, w1_ref, w2_ref, w2t_ref, kg_ref, kcmp_out, vcmpt_out):
    hk = _compress_hidden(kc_ref, pe_ref, w1_ref, 0)
    kcmp_out[0, 0] = _head_norm(jnp.dot(hk, w2_ref[0], precision=HI, preferred_element_type=F32), kg_ref[...])
    hv = _compress_hidden(vc_ref, pe_ref, w1_ref, 1)
    vcmpt_out[0, 0] = _nt_dot(w2t_ref[1], hv, precision=HI)


def _nsa_cmp(kc, vc, pe, w1, w2, kn_g0):
    batch, hk, seq, d = kc.shape
    nc = seq // CMP_STRIDE
    cw = CMP_STRIDE * d
    chunks = lambda t: t.reshape(batch, hk, nc, cw)
    blk = pl.BlockSpec((1, 1, nc, cw), lambda b, h: (b, h, 0, 0))
    full = lambda a: pl.BlockSpec(a.shape, lambda b, h: (0,) * a.ndim)
    pe2 = pe.reshape(2, 2, cw)
    w1h = w1.reshape(2, 2, cw, CMP_HIDDEN)
    w2t = w2.transpose(0, 2, 1)
    kg = kn_g0.reshape(1, d)
    return pl.pallas_call(
        _nsa_cmp_kernel,
        grid=(batch, hk),
        in_specs=[blk, blk, full(pe2), full(w1h), full(w2), full(w2t), full(kg)],
        out_specs=[pl.BlockSpec((1, 1, nc, d), lambda b, h: (b, h, 0, 0)),
                   pl.BlockSpec((1, 1, d, nc), lambda b, h: (b, h, 0, 0))],
        out_shape=[jax.ShapeDtypeStruct((batch, hk, nc, d), F32), jax.ShapeDtypeStruct((batch, hk, d, nc), F32)],
        compiler_params=_cparams("parallel", "parallel"),
        name="nsa_cmp",
    )(chunks(kc), chunks(vc), pe2, w1h, w2, w2t, kg)


def _softmax_cols(s, mask):
    s = jnp.where(mask, s, NEG_INF)
    e = jnp.where(mask, jnp.exp(s - jnp.max(s, axis=0, keepdims=True)), 0.0)
    den = jnp.sum(e, axis=0, keepdims=True)
    return e / jnp.where(den > 0, den, 1.0)


def _online_update(carry, s, mask, vt):
    mx, den, acc = carry
    s = jnp.where(mask, s, NEG_INF)
    mx_new = jnp.maximum(mx, jnp.max(s, axis=0, keepdims=True))
    alpha = jnp.exp(mx - mx_new)
    p = jnp.where(mask, jnp.exp(s - mx_new), 0.0)
    den = alpha * den + jnp.sum(p, axis=0, keepdims=True)
    acc = alpha * acc + jnp.dot(vt, p.astype(BF16), preferred_element_type=F32)
    return mx_new, den, acc


def _finish(carry):
    _, den, acc = carry
    return acc / jnp.where(den > 0, den, 1.0)


def _nsa_attn_kernel(qn_ref, qr_ref, kcmp_ref, vcmpt_ref, ks_ref, vst_ref, kw_ref, vwt_ref, gl_ref, ovl_ref,
                     o_ref, sel_ref, *, n_cmp, n_top):
    tq, g, d = ATT_TILE, NSA_GROUP, HEAD_DIM
    m = g * tq
    qt = pl.program_id(2)
    scale = d ** -0.5
    lane = lax.broadcasted_iota(jnp.int32, (1, m), 1)
    qpos = qt * tq + (lane & (tq - 1))
    qpos1 = qpos[:, :tq]

    qn = qn_ref[0].reshape(m, d)
    st = _nt_dot(kcmp_ref[0, 0], qn, precision=HI) * scale
    nc = st.shape[0]
    n_idx = lax.broadcasted_iota(jnp.int32, (nc, 1), 0)
    cmask = (n_idx * CMP_STRIDE + (CMP_LEN - 1) <= qpos) & (n_idx < n_cmp)
    pt = _softmax_cols(st, cmask)
    o_cmp = jnp.dot(vcmpt_ref[0, 0], pt, precision=HI, preferred_element_type=F32)
    psum = pt[:, :tq]
    for gi in range(1, g):
        psum = psum + pt[:, gi * tq:(gi + 1) * tq]
    imp = jnp.dot(ovl_ref[...], psum, precision=HI, preferred_element_type=F32)
    n_sel = imp.shape[0]
    j = lax.broadcasted_iota(jnp.int32, (n_sel, 1), 0)
    cur = qpos1 // SEL_BLOCK
    forced = (j == 0) | (j == cur) | (j == cur - 1)
    imp = jnp.where(forced, FORCE_SCORE, imp)
    imp = jnp.where(j > cur, -FORCE_SCORE, imp)
    rank = jnp.zeros((n_sel, tq), F32)
    for jp in range(n_sel):
        other = imp[jp:jp + 1, :]
        ahead = (other > imp) | ((other == imp) & (jp < j))
        rank = rank + jnp.where(ahead, 1.0, 0.0)
    sel_ref[...] = jnp.where((rank < n_top) & (j <= cur), 1.0, 0.0)

    qr = qr_ref[0].reshape(m, d)
    row = lax.broadcasted_iota(jnp.int32, (tq, 1), 0)
    init = (jnp.full((1, m), NEG_INF, F32), jnp.zeros((1, m), F32), jnp.zeros((d, m), F32))

    def sel_body(kt, carry):
        s = _nt_dot(ks_ref[0, 0, kt], qr) * scale
        blocks_per_tile = tq // SEL_BLOCK
        srow = sel_ref[pl.ds(kt * blocks_per_tile, 1), :]
        for bi in range(1, blocks_per_tile):
            srow = jnp.where(row < bi * SEL_BLOCK, srow, sel_ref[pl.ds(kt * blocks_per_tile + bi, 1), :])
        chosen = jnp.concatenate([srow] * g, axis=1) > 0.0
        kpos = kt * tq + row
        return _online_update(carry, s, chosen & (kpos <= qpos), vst_ref[0, 0, kt])

    o_sel = _finish(lax.fori_loop(0, qt + 1, sel_body, init))

    carry = init
    for back in range(WINDOW // tq, -1, -1):
        kt = qt - back
        ktc = jnp.maximum(kt, 0)
        s = _nt_dot(kw_ref[0, 0, ktc], qr) * scale
        kpos = kt * tq + row
        wmask = (kpos <= qpos) & (kpos > qpos - WINDOW) & (kpos >= 0)
        carry = _online_update(carry, s, wmask, vwt_ref[0, 0, ktc])
    o_win = _finish(carry)

    gates = _sigmoid(gl_ref[0, 0, 0])
    o_ref[0, 0, 0] = gates[0:1, :] * o_cmp + gates[1:2, :] * o_sel + gates[2:3, :] * o_win


def _overlap_t(seq):
    nc = seq // CMP_STRIDE
    n_cmp = (seq - CMP_LEN) // CMP_STRIDE + 1
    n_sel = seq // SEL_BLOCK
    cs = np.arange(nc) * CMP_STRIDE
    ss = np.arange(n_sel) * SEL_BLOCK
    ov = (cs[None, :] < ss[:, None] + SEL_BLOCK) & (cs[None, :] + CMP_LEN > ss[:, None]) & (np.arange(nc)[None, :] < n_cmp)
    return jnp.asarray(ov.astype(np.float32)), n_cmp, n_sel


def _nsa_attn(qn, qr, kcmp, vcmpt, ks, vst, kw, vwt, gl5):
    batch, _, seq, d = qn.shape
    hk, g, tq = NSA_KV_HEADS, NSA_GROUP, ATT_TILE
    nqt = seq // tq
    nc = seq // CMP_STRIDE
    ovl, n_cmp, n_sel = _overlap_t(seq)
    qblk = pl.BlockSpec((1, g, tq, d), lambda b, h, q: (b, h, q, 0))
    per_bh = lambda shape: pl.BlockSpec((1, 1) + shape, lambda b, h, q: (b, h) + (0,) * len(shape))
    tile5 = pl.BlockSpec((1, 1, 1, 8, g * tq), lambda b, h, q: (b, h, q, 0, 0))
    return pl.pallas_call(
        functools.partial(_nsa_attn_kernel, n_cmp=n_cmp, n_top=min(SEL_TOPK, n_sel)),
        grid=(batch, hk, nqt),
        in_specs=[qblk, qblk, per_bh((nc, d)), per_bh((d, nc)),
                  per_bh((nqt, tq, d)), per_bh((nqt, d, tq)), per_bh((nqt, tq, d)), per_bh((nqt, d, tq)),
                  tile5, pl.BlockSpec(ovl.shape, lambda b, h, q: (0, 0))],
        out_specs=pl.BlockSpec((1, 1, 1, d, g * tq), lambda b, h, q: (b, h, q, 0, 0)),
        out_shape=jax.ShapeDtypeStruct((batch, hk, nqt, d, g * tq), F32),
        scratch_shapes=[pltpu.VMEM((n_sel, tq), F32)],
        compiler_params=_cparams("parallel", "parallel", "arbitrary"),
        name="nsa_attn",
    )(qn, qr, kcmp, vcmpt, ks, vst, kw, vwt, gl5, ovl)


def _key_tiles(k):
    b, h, s, d = k.shape
    return k.reshape(b, h, s // ATT_TILE, ATT_TILE, d)


def _value_tiles_t(v2, batch, seq):
    v = v2.reshape(batch, seq // ATT_TILE, ATT_TILE, NSA_KV_HEADS, HEAD_DIM)
    return v.transpose(0, 3, 1, 4, 2).astype(BF16)


def _gate_tiles(gl, batch, seq):
    tq, g = ATT_TILE, NSA_GROUP
    x = gl[:, :3 * NSA_HEADS].reshape(batch, seq // tq, tq, NSA_KV_HEADS, g, 3)
    x = x.transpose(0, 3, 1, 5, 4, 2).reshape(batch, NSA_KV_HEADS, seq // tq, 3, g * tq)
    return jnp.pad(x, ((0, 0), (0, 0), (0, 0), (0, 5), (0, 0)))


def _attn_out_to_rows(o5, batch, seq):
    tq, g = ATT_TILE, NSA_GROUP
    x = o5.reshape(batch, NSA_KV_HEADS, seq // tq, HEAD_DIM, g, tq)
    return x.transpose(0, 2, 5, 1, 4, 3).reshape(batch * seq, NSA_WIDTH)


def _outproj_kernel(x_ref, yrw_ref, g_ref, ynsa_ref, wo_ref, ln2_ref, wr_ref, br_ref,
                    x1_out, h_out, te_out, tg_out):
    c = RW_WIDTH
    yr = (yrw_ref[...] * g_ref[...]).astype(BF16)
    yn = ynsa_ref[...].astype(BF16)
    x1 = (x_ref[...] + jnp.dot(yr, wo_ref[:c, :], preferred_element_type=F32)
          + jnp.dot(yn, wo_ref[c:, :], preferred_element_type=F32))
    x1_out[...] = x1
    ms = jnp.mean(x1 * x1, axis=-1, keepdims=True)
    h = x1 * lax.rsqrt(ms + NORM_EPS) * ln2_ref[...]
    h_out[...] = h
    logits = jnp.dot(h, wr_ref[...], precision=HI, preferred_element_type=F32) + br_ref[...]
    lane = lax.broadcasted_iota(jnp.int32, logits.shape, 1)
    cur = jnp.where(lane < N_EXPERTS, logits, -jnp.inf)
    vals, idxs = [], []
    for _ in range(TOP_K):
        mx = jnp.max(cur, axis=-1, keepdims=True)
        idx = jnp.min(jnp.where(cur == mx, lane, ROUTE_PAD), axis=-1, keepdims=True)
        vals.append(mx)
        idxs.append(idx)
        cur = jnp.where(lane == idx, -jnp.inf, cur)
    es = [jnp.exp(v - vals[0]) for v in vals]
    den = es[0]
    for e in es[1:]:
        den = den + e
    te = jnp.zeros(logits.shape, jnp.int32)
    tg = jnp.zeros(logits.shape, F32)
    for k in range(TOP_K):
        te = jnp.where(lane == k, idxs[k], te)
        tg = jnp.where(lane == k, es[k] / den, tg)
    te_out[...] = te
    tg_out[...] = tg


def _outproj(x2, y_rw, g, y_nsa, w_out, ln2_g, wr_pad, br_pad, tm=512):
    t, d = x2.shape
    c = RW_WIDTH
    row = lambda n: pl.BlockSpec((tm, n), lambda i: (i, 0))
    full = lambda a: pl.BlockSpec(a.shape, lambda i: (0, 0))
    ln2 = ln2_g.reshape(1, d)
    return pl.pallas_call(
        _outproj_kernel,
        grid=(t // tm,),
        in_specs=[row(d), row(c), row(c), row(c), full(w_out), full(ln2), full(wr_pad), full(br_pad)],
        out_specs=[row(d), row(d), row(ROUTE_PAD), row(ROUTE_PAD)],
        out_shape=[jax.ShapeDtypeStruct((t, d), F32), jax.ShapeDtypeStruct((t, d), F32),
                   jax.ShapeDtypeStruct((t, ROUTE_PAD), jnp.int32), jax.ShapeDtypeStruct((t, ROUTE_PAD), F32)],
        compiler_params=_cparams("parallel"),
        name="outproj_router",
    )(x2, y_rw, g, y_nsa, w_out, ln2, wr_pad, br_pad)


def _route_tables(top_e, t, bm):
    n = t * TOP_K
    flat_e = top_e.reshape(n)
    skey = jnp.sort(flat_e * n + jnp.arange(n, dtype=jnp.int32))
    starts = jnp.searchsorted(skey, jnp.arange(N_EXPERTS + 1, dtype=jnp.int32) * n).astype(jnp.int32)
    counts = starts[1:] - starts[:-1]
    padded = (counts + bm - 1) // bm * bm
    pad_end = jnp.cumsum(padded)
    pad_start = pad_end - padded
    n_blk = n // bm + N_EXPERTS
    blk_e = jnp.minimum(jnp.searchsorted(pad_end, jnp.arange(n_blk, dtype=jnp.int32) * bm, side='right'),
                        N_EXPERTS - 1).astype(jnp.int32)
    e_p = jnp.repeat(blk_e, bm)
    off = jnp.arange(n_blk * bm, dtype=jnp.int32) - pad_start[e_p]
    valid = off < counts[e_p]
    sid = skey[jnp.clip(starts[e_p] + off, 0, n - 1)] % n
    tok = jnp.where(valid, sid // TOP_K, 0)
    spare = n + jnp.cumsum(jnp.where(valid, 0, 1)) - 1
    dst = jnp.where(valid, (sid % TOP_K) * t + tok, spare)
    shape3 = (n_blk, 1, bm)
    return blk_e, tok.astype(jnp.int32).reshape(shape3), dst.astype(jnp.int32).reshape(shape3)


def _moe_kernel(blk_e_ref, tok_ref, tok_next_ref, dst_ref, h_hbm, w1_ref, b1_ref, w2_ref, b2_ref, out_hbm,
                xbuf, ybuf, gsem, ssem):
    del blk_e_ref
    bm = xbuf.shape[1]
    d_ff = w2_ref.shape[1]
    b = pl.program_id(0)
    nb = pl.num_programs(0)
    slot = b % 2

    def start_gather(idx_ref, s):
        def body(r, c):
            pltpu.make_async_copy(h_hbm.at[pl.ds(idx_ref[0, 0, r], 1), :], xbuf.at[s, pl.ds(r, 1), :],
                                  gsem.at[s]).start()
            return c
        lax.fori_loop(0, bm, body, 0)

    def wait_rows_in(s):
        pltpu.make_async_copy(h_hbm.at[pl.ds(0, bm), :], xbuf.at[s], gsem.at[s]).wait()

    def wait_rows_out(s):
        pltpu.make_async_copy(ybuf.at[s], out_hbm.at[pl.ds(0, bm), :], ssem.at[s]).wait()

    @pl.when(b == 0)
    def _():
        start_gather(tok_ref, 0)

    @pl.when(b + 1 < nb)
    def _():
        start_gather(tok_next_ref, 1 - slot)

    wait_rows_in(slot)

    @pl.when(b >= 2)
    def _():
        wait_rows_out(slot)

    x = xbuf[slot].astype(BF16)
    u = jnp.dot(x, w1_ref[0], preferred_element_type=F32) + b1_ref[0]
    glu = jnp.minimum(u[:, :d_ff], SWIGLU_LIMIT)
    lin = jnp.clip(u[:, d_ff:], -SWIGLU_LIMIT, SWIGLU_LIMIT)
    act = glu * _sigmoid(SWIGLU_ALPHA * glu) * (lin + 1.0)
    ybuf[slot] = jnp.dot(act.astype(BF16), w2_ref[0], preferred_element_type=F32) + b2_ref[0]

    def scatter(r, c):
        pltpu.make_async_copy(ybuf.at[slot, pl.ds(r, 1), :], out_hbm.at[pl.ds(dst_ref[0, 0, r], 1), :],
                              ssem.at[slot]).start()
        return c
    lax.fori_loop(0, bm, scatter, 0)

    @pl.when(b == nb - 1)
    def _():
        wait_rows_out(slot)

        @pl.when(nb >= 2)
        def _():
            wait_rows_out(1 - slot)


def _moe_ffn(h, blk_e, tok3, dst3, w1, b1, w2, b2):
    t, d = h.shape
    n_blk, _, bm = tok3.shape
    ne, _, ff2 = w1.shape
    smem_blk = lambda f: pl.BlockSpec((1, 1, bm), f, memory_space=pltpu.SMEM)
    by_expert = lambda shape: pl.BlockSpec((1,) + shape, lambda b, e: (e[b], 0, 0))
    grid_spec = pltpu.PrefetchScalarGridSpec(
        num_scalar_prefetch=1,
        grid=(n_blk,),
        in_specs=[smem_blk(lambda b, e: (b, 0, 0)),
                  smem_blk(lambda b, e: (jnp.minimum(b + 1, n_blk - 1), 0, 0)),
                  smem_blk(lambda b, e: (b, 0, 0)),
                  pl.BlockSpec(memory_space=pl.ANY),
                  by_expert((d, ff2)), by_expert((1, ff2)), by_expert((ff2 // 2, d)), by_expert((1, d))],
        out_specs=pl.BlockSpec(memory_space=pl.ANY),
        scratch_shapes=[pltpu.VMEM((2, bm, d), F32), pltpu.VMEM((2, bm, d), F32),
                        pltpu.SemaphoreType.DMA((2,)), pltpu.SemaphoreType.DMA((2,))],
    )
    return pl.pallas_call(
        _moe_kernel,
        grid_spec=grid_spec,
        out_shape=jax.ShapeDtypeStruct((n_blk * bm, d), F32),
        compiler_params=_cparams("arbitrary"),
        name="moe_ffn",
    )(blk_e, tok3, tok3, dst3, h, w1, b1.reshape(ne, 1, ff2), w2, b2.reshape(ne, 1, d))


def _combine_kernel(x1_ref, g_ref, y0_ref, y1_ref, y2_ref, y3_ref, o_ref):
    g = g_ref[...]
    acc = x1_ref[...]
    for k, y_ref in enumerate((y0_ref, y1_ref, y2_ref, y3_ref)):
        acc = acc + g[:, k:k + 1] * y_ref[...]
    o_ref[...] = acc


def _combine(x1, gates, y_rows, tm=512):
    t, d = x1.shape
    nt = t // tm
    row = lambda n: pl.BlockSpec((tm, n), lambda i: (i, 0))
    slot = lambda k: pl.BlockSpec((tm, d), lambda i: (k * nt + i, 0))
    return pl.pallas_call(
        _combine_kernel,
        grid=(nt,),
        in_specs=[row(d), row(ROUTE_PAD)] + [slot(k) for k in range(TOP_K)],
        out_specs=row(d),
        out_shape=jax.ShapeDtypeStruct((t, d), F32),
        compiler_params=_cparams("parallel"),
        name="moe_combine",
    )(x1, gates, y_rows, y_rows, y_rows, y_rows)


def _layer(x, ln1_g, w_in, mu_rkv, mu_wag, w0, w1, w2, a0, a1, a2, g1, g2, k_k, k_a, r_k, lnx_g, lnx_b,
           qn_g, kn_g, pe, cmp_w1, cmp_w2, w_out, ln2_g, wr, br, e_w1, e_b1, e_w2, e_b2):
    batch, seq, d = x.shape
    t = batch * seq
    x2 = x.reshape(t, d)

    lora_w = jnp.concatenate([w1, a1, g1], axis=1)
    mu_cols = jnp.concatenate([jnp.broadcast_to(mu_wag[0][:, None], w1.shape),
                               jnp.broadcast_to(mu_wag[1][:, None], a1.shape),
                               jnp.broadcast_to(mu_wag[2][:, None], g1.shape)], axis=1)
    n_named = 3 * RW_WIDTH + NSA_WIDTH + 6 * NSA_KV_WIDTH
    w_all = jnp.concatenate([w_in[:, :n_named], jnp.pad(w_in[:, n_named:], ((0, 0), (0, GL_PAD - 3 * NSA_HEADS))),
                             (1.0 - mu_cols) * lora_w, mu_cols * lora_w], axis=1).astype(BF16)
    rkv, q, kv, gl, lora = _inproj(x2, ln1_g, w_all)

    r, w, k, v, kkr, a, g = _rwkv_prep(rkv, lora, batch, seq, mu_rkv, w0, w2, a0, a2, g2, k_k, k_a)
    to_scan = functools.partial(_to_scan_layout, batch=batch, seq=seq)
    y_scan = _rwkv_scan(to_scan(r), to_scan(w), to_scan(k), to_scan(v), to_scan(kkr), to_scan(a),
                        _head_param_to_lanes(r_k, batch), _head_param_to_lanes(lnx_g, batch),
                        _head_param_to_lanes(lnx_b, batch))
    y_rw = _from_scan_layout(y_scan, batch, seq)

    qn, qr, kc, vc, ks, kw = _nsa_prep(q, kv, batch, seq, qn_g, kn_g)
    kcmp, vcmpt = _nsa_cmp(kc, vc, pe, cmp_w1, cmp_w2, kn_g[0])
    kvw = NSA_KV_WIDTH
    o5 = _nsa_attn(qn, qr, kcmp, vcmpt, _key_tiles(ks), _value_tiles_t(kv[:, 3 * kvw:4 * kvw], batch, seq),
                   _key_tiles(kw), _value_tiles_t(kv[:, 5 * kvw:6 * kvw], batch, seq), _gate_tiles(gl, batch, seq))
    y_nsa = _attn_out_to_rows(o5, batch, seq)

    wr_pad = jnp.pad(wr, ((0, 0), (0, ROUTE_PAD - N_EXPERTS)))
    br_pad = jnp.pad(br, (0, ROUTE_PAD - N_EXPERTS)).reshape(1, ROUTE_PAD)
    x1, h, top_e, gates = _outproj(x2, y_rw, g, y_nsa, w_out.astype(BF16), ln2_g, wr_pad, br_pad)

    blk_e, tok3, dst3 = _route_tables(top_e[:, :TOP_K], t, MOE_ROWS)
    y_rows = _moe_ffn(h, blk_e, tok3, dst3, e_w1.astype(BF16), e_b1, e_w2.astype(BF16), e_b2)
    return _combine(x1, gates, y_rows).reshape(batch, seq, d)


def kernel(x, ln1_g, w_in, rw_mu_rkv, rw_mu_wag, rw_w0, rw_w1, rw_w2, rw_a0, rw_a1, rw_a2, rw_g1, rw_g2, rw_k_k, rw_k_a, rw_r_k, rw_lnx_g, rw_lnx_b, nsa_qn_g, nsa_kn_g, nsa_pe, nsa_cmp_w1, nsa_cmp_w2, w_out, ln2_g, moe_wr, moe_br, moe_w1, moe_b1, moe_w2, moe_b2):
    params = (ln1_g, w_in, rw_mu_rkv, rw_mu_wag, rw_w0, rw_w1, rw_w2, rw_a0, rw_a1, rw_a2, rw_g1, rw_g2, rw_k_k,
              rw_k_a, rw_r_k, rw_lnx_g, rw_lnx_b, nsa_qn_g, nsa_kn_g, nsa_pe, nsa_cmp_w1, nsa_cmp_w2, w_out, ln2_g,
              moe_wr, moe_br, moe_w1, moe_b1, moe_w2, moe_b2)
    for layer in range(ln1_g.shape[0]):
        x = _layer(x, *(p[layer] for p in params))
    return x
```
